```python
import math
import jax, jax.numpy as jnp
from jax import lax
import numpy as np

D_MODEL = 1024
BATCH = 8
SEQ = 4096
DEPTH = 1

D_MIX = D_MODEL
ATT_WIDTH = D_MIX // 2
RET_WIDTH = D_MIX - ATT_WIDTH
ATT_HEADS = 8
ATT_KV_HEADS = 2
ATT_HEAD_DIM = ATT_WIDTH // ATT_HEADS
IDX_HEADS = 4
IDX_DIM = 64
TOPK_MAX = 256
Q_BLOCK = 128
RET_HEADS = 4
RET_HEAD_DIM = RET_WIDTH // RET_HEADS
RET_CHUNK = 128
ROPE_BASE = 10000.0
N_BUCKETS = 32
MAX_DISTANCE = 128
N_EXPERTS = 32
TOP_K = 4
D_FF = D_MODEL
SWIGLU_ALPHA = 1.702
SWIGLU_LIMIT = 7.0
MOE_BLOCK = 256
LN_EPS = 1e-5
DN_ALPHA = (2 * DEPTH) ** 0.25
DN_BETA = (8 * DEPTH) ** -0.25
IN_SPLITS = (ATT_WIDTH, ATT_KV_HEADS * ATT_HEAD_DIM, ATT_KV_HEADS * ATT_HEAD_DIM,
             IDX_HEADS * IDX_DIM, IDX_DIM, IDX_HEADS,
             RET_WIDTH, RET_WIDTH, RET_WIDTH, RET_WIDTH)
D_IN = sum(IN_SPLITS)

kernel_name = "dsa_retention_hybrid_moe_deepnorm"


def layer_norm(x, g, b):
    xf = x.astype(jnp.float32)
    mu = jnp.mean(xf, -1, keepdims=True)
    var = jnp.mean(jnp.square(xf - mu), -1, keepdims=True)
    return ((xf - mu) * lax.rsqrt(var + LN_EPS)).astype(x.dtype) * g + b


def head_norm(y):
    yf = y.astype(jnp.float32)
    mu = jnp.mean(yf, -1, keepdims=True)
    var = jnp.mean(jnp.square(yf - mu), -1, keepdims=True)
    return ((yf - mu) * lax.rsqrt(var + LN_EPS)).astype(y.dtype)


def t5_bucket(dist):
    max_exact = N_BUCKETS // 2
    n = jnp.maximum(dist, 0)
    nf = jnp.maximum(n, 1).astype(jnp.float32)
    large = max_exact + (jnp.log(nf / max_exact) / math.log(MAX_DISTANCE / max_exact)
                         * (N_BUCKETS - max_exact)).astype(jnp.int32)
    large = jnp.minimum(large, N_BUCKETS - 1)
    return jnp.where(n < max_exact, n, large)


def rotary(x):
    L, d = x.shape[1], x.shape[-1]
    inv = 1.0 / (ROPE_BASE ** (jnp.arange(0, d, 2, dtype=jnp.float32) / d))
    ang = jnp.arange(L, dtype=jnp.float32)[:, None] * inv[None, :]
    cos = jnp.cos(ang)[None, :, None, :]
    sin = jnp.sin(ang)[None, :, None, :]
    xf = x.astype(jnp.float32)
    x1, x2 = xf[..., : d // 2], xf[..., d // 2:]
    return jnp.concatenate([x1 * cos - x2 * sin, x1 * sin + x2 * cos], -1).astype(x.dtype)


def dsa_attention(q, k, v, q_idx, k_idx, w_idx, rel_bias):
    B, L = q.shape[0], q.shape[1]
    k_sel = min(TOPK_MAX, L // 4)
    nqb = L // Q_BLOCK
    rep = ATT_HEADS // ATT_KV_HEADS
    key_pos = jnp.arange(L, dtype=jnp.int32)
    b_ix = jnp.arange(B)[:, None, None]

    def blocks(a):
        return jnp.moveaxis(a.reshape(B, nqb, Q_BLOCK, *a.shape[2:]), 1, 0)

    def one_block(args):
        qb, qib, wb, tpos = args
        rel = jax.nn.relu(jnp.einsum('bthd,bsd->bths', qib, k_idx) * IDX_DIM ** -0.5)
        score = jnp.einsum('bths,bth->bts', rel, wb).astype(jnp.float32)
        causal = key_pos[None, :] <= tpos[:, None]
        score = jnp.where(causal[None], score, -jnp.inf)
        _, idx = lax.top_k(score, k_sel)
        ks = k[b_ix, idx]
        vs = v[b_ix, idx]
        qg = qb.reshape(B, Q_BLOCK, ATT_KV_HEADS, rep, ATT_HEAD_DIM)
        logits = jnp.einsum('btgrd,btkgd->btgrk', qg, ks).astype(jnp.float32) * ATT_HEAD_DIM ** -0.5
        dist = tpos[None, :, None] - idx
        bias = rel_bias[t5_bucket(dist)].reshape(B, Q_BLOCK, k_sel, ATT_KV_HEADS, rep)
        logits = logits + jnp.moveaxis(bias, 2, -1).astype(jnp.float32)
        valid = (dist >= 0)[:, :, None, None, :]
        logits = jnp.where(valid, logits, -jnp.inf)
        p = jax.nn.softmax(logits, axis=-1).astype(v.dtype)
        o = jnp.einsum('btgrk,btkgd->btgrd', p, vs)
        return o.reshape(B, Q_BLOCK, ATT_WIDTH)

    out = lax.map(one_block, (blocks(q), blocks(q_idx), blocks(w_idx),
                              key_pos.reshape(nqb, Q_BLOCK)))
    return jnp.moveaxis(out, 0, 1).reshape(B, L, ATT_WIDTH)


def retention(q, k, v):
    B, L, H, d = q.shape
    nc = L // RET_CHUNK
    log_g = jnp.log(1.0 - 2.0 ** (-5.0 - jnp.arange(H, dtype=jnp.float32)))
    i = jnp.arange(RET_CHUNK, dtype=jnp.float32)
    diff = i[:, None] - i[None, :]
    inner_decay = jnp.where(diff[None] >= 0,
                            jnp.exp(jnp.maximum(diff, 0.0)[None] * log_g[:, None, None]), 0.0)
    q_decay = jnp.exp((i[None, :] + 1.0) * log_g[:, None])[None, :, :, None]
    k_decay = jnp.exp((RET_CHUNK - 1.0 - i[None, :]) * log_g[:, None])[None, :, :, None]
    chunk_decay = jnp.exp(RET_CHUNK * log_g)[None, :, None, None]

    def to_chunks(a):
        return a.astype(jnp.float32).reshape(B, nc, RET_CHUNK, H, d).transpose(1, 0, 3, 2, 4)

    def step(state, inp):
        qi, ki, vi = inp
        attn = jnp.einsum('bhid,bhjd->bhij', qi, ki) * inner_decay[None]
        inner = jnp.einsum('bhij,bhjv->bhiv', attn, vi)
        cross = jnp.einsum('bhid,bhdv->bhiv', qi, state) * q_decay
        new_state = state * chunk_decay + jnp.einsum('bhjd,bhjv->bhdv', ki * k_decay, vi)
        return new_state, inner + cross

    state0 = jnp.zeros((B, H, d, d), jnp.float32)
    _, out = lax.scan(step, state0, (to_chunks(q), to_chunks(k), to_chunks(v)))
    return out.transpose(1, 0, 3, 2, 4).reshape(B, L, H, d).astype(q.dtype)


def hybrid_mixer(h, w_in, ret_norm_g, w_out, rel_bias):
    B, L, _ = h.shape
    proj = h @ w_in
    cuts = np.cumsum(IN_SPLITS)[:-1].tolist()
    q_a, k_a, v_a, q_i, k_i, w_i, q_r, k_r, v_r, g_r = jnp.split(proj, cuts, axis=-1)
    attn = dsa_attention(
        q_a.reshape(B, L, ATT_HEADS, ATT_HEAD_DIM),
        k_a.reshape(B, L, ATT_KV_HEADS, ATT_HEAD_DIM),
        v_a.reshape(B, L, ATT_KV_HEADS, ATT_HEAD_DIM),
        q_i.reshape(B, L, IDX_HEADS, IDX_DIM), k_i, w_i * IDX_HEADS ** -0.5, rel_bias)
    qr = rotary(q_r.reshape(B, L, RET_HEADS, RET_HEAD_DIM))
    kr = rotary(k_r.reshape(B, L, RET_HEADS, RET_HEAD_DIM)) * RET_HEAD_DIM ** -0.5
    ret = retention(qr, kr, v_r.reshape(B, L, RET_HEADS, RET_HEAD_DIM))
    ret = head_norm(ret).reshape(B, L, RET_WIDTH) * ret_norm_g
    ret = jax.nn.silu(g_r) * ret
    return jnp.concatenate([attn, ret], axis=-1) @ w_out


def clamped_swiglu(gu):
    g, u = gu[..., :D_FF], gu[..., D_FF:]
    g = jnp.minimum(g, SWIGLU_LIMIT)
    u = jnp.clip(u, -SWIGLU_LIMIT, SWIGLU_LIMIT)
    return (u + 1.0) * (g * jax.nn.sigmoid(g * SWIGLU_ALPHA))


def moe(h, w_router, b_router, w_gate_up, b_gate_up, w_down, b_down):
    B, L, D = h.shape
    N = B * L
    hf = h.reshape(N, D)
    logits = hf @ w_router + b_router
    top_val, top_idx = lax.top_k(logits, TOP_K)
    gates = jax.nn.softmax(top_val.astype(jnp.float32), axis=-1).astype(h.dtype)
    e_flat = top_idx.reshape(-1).astype(jnp.int32)
    tok_flat = jnp.arange(N * TOP_K, dtype=jnp.int32) // TOP_K
    g_flat = gates.reshape(-1)
    order = jnp.argsort(e_flat)
    e_sorted = e_flat[order]
    counts = jnp.zeros((N_EXPERTS,), jnp.int32).at[e_flat].add(1)
    starts = jnp.cumsum(counts) - counts
    padded = (counts + MOE_BLOCK - 1) // MOE_BLOCK * MOE_BLOCK
    pends = jnp.cumsum(padded)
    pstarts = pends - padded
    rank = jnp.arange(N * TOP_K, dtype=jnp.int32) - starts[e_sorted]
    dest = pstarts[e_sorted] + rank
    n_blocks = -(-(N * TOP_K) // MOE_BLOCK) + N_EXPERTS
    P = n_blocks * MOE_BLOCK
    row_tok = jnp.full((P,), N, jnp.int32).at[dest].set(tok_flat[order])
    row_gate = jnp.zeros((P,), h.dtype).at[dest].set(g_flat[order])
    block_start = jnp.arange(n_blocks, dtype=jnp.int32) * MOE_BLOCK
    block_exp = jnp.minimum(jnp.searchsorted(pends, block_start, side='right'),
                            N_EXPERTS - 1).astype(jnp.int32)

    def one_block(args):
        tok, gate, e = args
        xb = jnp.take(hf, tok, axis=0, mode='fill', fill_value=0)
        gu = xb @ w_gate_up[e] + b_gate_up[e]
        y = clamped_swiglu(gu) @ w_down[e] + b_down[e]
        return y * gate[:, None]

    ys = lax.map(one_block, (row_tok.reshape(n_blocks, MOE_BLOCK),
                             row_gate.reshape(n_blocks, MOE_BLOCK), block_exp))
    out = jnp.zeros((N, D), h.dtype).at[row_tok].add(ys.reshape(P, D), mode='drop')
    return out.reshape(B, L, D)


def setup_inputs(seed: int = 0) -> dict:
    key = jax.random.key(seed)
    ks = jax.random.split(key, 20)
    f32 = jnp.float32
    D = D_MODEL
    nrm = lambda k, s: jax.random.normal(k, s, f32)
    col_scale = np.ones((D_IN,), np.float32)
    offs = np.concatenate([[0], np.cumsum(IN_SPLITS)])
    for j in (2, 8):
        col_scale[offs[j]:offs[j + 1]] = DN_BETA
    w_in = nrm(ks[3], (DEPTH, D, D_IN)) * D ** -0.5 * jnp.asarray(col_scale)
    return {
        "x": nrm(ks[0], (BATCH, SEQ, D)),
        "c": nrm(ks[1], (BATCH, D)),
        "rel_bias": nrm(ks[2], (N_BUCKETS, ATT_HEADS)) * 0.5,
        "w_ada": nrm(ks[4], (DEPTH, D, 6 * D)) * D ** -0.5 * 0.2,
        "b_ada": nrm(ks[5], (DEPTH, 6 * D)) * 0.01,
        "w_in": w_in,
        "ret_norm_g": 1.0 + 0.05 * nrm(ks[6], (DEPTH, RET_WIDTH)),
        "w_out": nrm(ks[7], (DEPTH, D_MIX, D)) * D_MIX ** -0.5 * DN_BETA,
        "ln1_g": 1.0 + 0.05 * nrm(ks[8], (DEPTH, D)),
        "ln1_b": 0.01 * nrm(ks[9], (DEPTH, D)),
        "w_router": nrm(ks[10], (DEPTH, D, N_EXPERTS)) * D ** -0.5,
        "b_router": 0.01 * nrm(ks[11], (DEPTH, N_EXPERTS)),
        "w_gate_up": nrm(ks[12], (DEPTH, N_EXPERTS, D, 2 * D_FF)) * D ** -0.5 * DN_BETA,
        "b_gate_up": 0.01 * nrm(ks[13], (DEPTH, N_EXPERTS, 2 * D_FF)),
        "w_down": nrm(ks[14], (DEPTH, N_EXPERTS, D_FF, D)) * D_FF ** -0.5 * DN_BETA,
        "b_down": 0.01 * nrm(ks[15], (DEPTH, N_EXPERTS, D)),
        "ln2_g": 1.0 + 0.05 * nrm(ks[16], (DEPTH, D)),
        "ln2_b": 0.01 * nrm(ks[17], (DEPTH, D)),
    }


def reference(x, c, rel_bias, w_ada, b_ada, w_in, ret_norm_g, w_out, ln1_g, ln1_b,
              w_router, b_router, w_gate_up, b_gate_up, w_down, b_down, ln2_g, ln2_b):
    for l in range(DEPTH):
        mod = jax.nn.silu(c) @ w_ada[l] + b_ada[l]
        sh_a, sc_a, g_a, sh_f, sc_f, g_f = jnp.split(mod[:, None, :], 6, axis=-1)
        h = x * (1.0 + sc_a) + sh_a
        mix = hybrid_mixer(h, w_in[l], ret_norm_g[l], w_out[l], rel_bias)
        x = layer_norm(DN_ALPHA * x + (1.0 + g_a) * mix, ln1_g[l], ln1_b[l])
        h = x * (1.0 + sc_f) + sh_f
        ff = moe(h, w_router[l], b_router[l], w_gate_up[l], b_gate_up[l], w_down[l], b_down[l])
        x = layer_norm(DN_ALPHA * x + (1.0 + g_f) * ff, ln2_g[l], ln2_b[l])
    return x
```

```python
import functools
import math

import numpy as np
import jax
import jax.numpy as jnp
from jax import lax
from jax.experimental import pallas as pl
from jax.experimental.pallas import tpu as pltpu

F32 = jnp.float32
BF16 = jnp.bfloat16
I32 = jnp.int32
HIGHEST = lax.Precision.HIGHEST

ATT_HEADS = 8
ATT_KV_HEADS = 2
ATT_HEAD_DIM = 64
ATT_REP = ATT_HEADS // ATT_KV_HEADS
IDX_HEADS = 4
IDX_DIM = 64
TOPK_MAX = 256
Q_BLOCK = 128
RET_HEADS = 4
RET_HEAD_DIM = 128
RET_CHUNK = 128
ROPE_BASE = 10000.0
N_BUCKETS = 32
MAX_DISTANCE = 128
N_EXPERTS = 32
TOP_K = 4
SWIGLU_ALPHA = 1.702
SWIGLU_LIMIT = 7.0
LN_EPS = 1e-5
DEPTH = 1
DN_ALPHA = (2 * DEPTH) ** 0.25

LANES = 128
SUBLANES = 8
VMEM_LIMIT = 56 * 1024 * 1024

KEY_CHUNK = 256
MOE_ROWS = 512
NEG = -1e30
INT_MIN = -2 ** 31

_C_QR, _C_KR, _C_VR, _C_GR, _C_QA, _C_KA, _C_VA, _C_QI, _C_KI = (
    0, 512, 1024, 1536, 2048, 2560, 2688, 2816, 3072)
_C_TOTAL = 3200


def _cparams(sem):
    return pltpu.CompilerParams(dimension_semantics=sem, vmem_limit_bytes=VMEM_LIMIT)


def _ada_kernel(c_ref, w_ref, b_ref, o_ref):
    c = c_ref[...]
    cs = c * jax.nn.sigmoid(c)
    o_ref[...] = jnp.dot(cs, w_ref[...], precision=HIGHEST,
                         preferred_element_type=F32) + b_ref[...]


def _ada(c, w_ada, b_ada):
    B, D = c.shape
    n = w_ada.shape[1] // D
    return pl.pallas_call(
        _ada_kernel,
        grid=(n,),
        in_specs=[pl.BlockSpec((B, D), lambda j: (0, 0)),
                  pl.BlockSpec((D, D), lambda j: (0, j)),
                  pl.BlockSpec((1, D), lambda j: (0, j))],
        out_specs=pl.BlockSpec((B, D), lambda j: (0, j)),
        out_shape=jax.ShapeDtypeStruct((B, n * D), F32),
        compiler_params=_cparams(("arbitrary",)),
        name="ada",
    )(c, w_ada, b_ada.reshape(1, -1))


def _inproj_kernel(x_ref, sc_ref, sh_ref, w_ref, o_ref, h_s):
    @pl.when(pl.program_id(1) == 0)
    def _():
        h_s[...] = (x_ref[...] * (1.0 + sc_ref[...]) + sh_ref[...]).astype(BF16)

    o_ref[...] = jnp.dot(h_s[...], w_ref[...], preferred_element_type=F32)


def _inproj(xf, sc, sh, w_cat, L, tm=512, tn=640):
    N, D = xf.shape
    C = w_cat.shape[1]
    return pl.pallas_call(
        _inproj_kernel,
        grid=(N // tm, C // tn),
        in_specs=[pl.BlockSpec((tm, D), lambda i, j: (i, 0)),
                  pl.BlockSpec((None, 1, D), lambda i, j: (i * tm // L, 0, 0)),
                  pl.BlockSpec((None, 1, D), lambda i, j: (i * tm // L, 0, 0)),
                  pl.BlockSpec((D, tn), lambda i, j: (0, j))],
        out_specs=pl.BlockSpec((tm, tn), lambda i, j: (i, j)),
        out_shape=jax.ShapeDtypeStruct((N, C), F32),
        scratch_shapes=[pltpu.VMEM((tm, D), BF16)],
        compiler_params=_cparams(("arbitrary", "arbitrary")),
        name="inproj",
    )(xf, sc, sh, w_cat)


def _dsa_kernel(qa_ref, qi_ref, wq_ref, kv_ref, ki_ref, bias_ref, tri_ref, o_ref,
                kbf_s, vext_s, kibf_s, key_s, mb_s, m_s, acc_s, *, ksel, n_chunks):
    qb = pl.program_id(1)
    CW = KEY_CHUNK
    T = Q_BLOCK
    dh = ATT_HEAD_DIM

    @pl.when(qb == 0)
    def _prep():
        def body(c, carry):
            rows = pl.ds(pl.multiple_of(c * CW, CW), CW)
            kv = kv_ref[rows, :]
            for g in range(ATT_KV_HEADS):
                kbf_s[g, rows, :] = kv[:, g * dh:(g + 1) * dh].astype(BF16)
                vg = kv[:, LANES + g * dh:LANES + (g + 1) * dh].astype(BF16)
                vext_s[g, rows, :] = jnp.concatenate(
                    [vg, jnp.ones((CW, LANES - dh), BF16)], axis=1)
            kibf_s[rows, :] = ki_ref[rows, 0:IDX_DIM].astype(BF16)
            return carry
        lax.fori_loop(0, n_chunks, body, 0)

    n_all = qb // 2 + 1
    row_pos = qb * T + lax.broadcasted_iota(I32, (T, 1), 0)
    col_iota = lax.broadcasted_iota(I32, (1, CW), 1)

    qi = qi_ref[...].astype(BF16)
    qih = [qi[:, h * IDX_DIM:(h + 1) * IDX_DIM] for h in range(IDX_HEADS)]
    wq = wq_ref[:, IDX_DIM:IDX_DIM + IDX_HEADS] * (IDX_HEADS ** -0.5)
    wqh = [jnp.broadcast_to(wq[:, h:h + 1], (T, CW)) for h in range(IDX_HEADS)]

    def score_body(c, carry):
        rows = pl.ds(pl.multiple_of(c * CW, CW), CW)
        kic = kibf_s[rows, :]
        sc = jnp.zeros((T, CW), F32)
        for h in range(IDX_HEADS):
            d = lax.dot_general(qih[h], kic, (((1,), (1,)), ((), ())),
                                preferred_element_type=F32)
            sc = sc + jnp.maximum(d * (IDX_DIM ** -0.5), 0.0) * wqh[h]
        bits = pltpu.bitcast(sc + 0.0, I32)
        key = bits ^ ((bits >> 31) & jnp.int32(0x7FFFFFFF))
        causal = (c * CW + col_iota) <= row_pos
        key_s[c] = jnp.where(causal, key, jnp.int32(INT_MIN))
        return carry
    lax.fori_loop(0, n_all, score_body, 0)

    def count(pred_fn):
        def body(c, cnt):
            p = pred_fn(key_s[c]).astype(I32)
            return cnt + p[:, :LANES] + p[:, LANES:]
        cnt = lax.fori_loop(0, n_all, body, jnp.zeros((T, LANES), I32))
        return jnp.sum(cnt, axis=1, keepdims=True)

    def bit_body(it, prefix):
        cand = prefix | lax.shift_left(jnp.int32(1), 31 - it)
        thr_c = cand ^ jnp.int32(INT_MIN)
        total = count(lambda k: k >= thr_c)
        return jnp.where(total >= ksel, cand, prefix)
    prefix = lax.fori_loop(0, 32, bit_body, jnp.zeros((T, 1), I32))
    thr = prefix ^ jnp.int32(INT_MIN)

    n_gt = count(lambda k: k > thr)
    m_eq = (ksel - n_gt).astype(F32)
    tri = tri_ref[...]

    def mask_body(c, carry):
        k = key_s[c]
        gt = k > thr
        eq = k == thr
        eqf = jnp.where(eq, 1.0, 0.0)
        rank = carry + jnp.dot(eqf.astype(BF16), tri, preferred_element_type=F32)
        sel = gt | (eq & (rank < m_eq))
        sel = sel & ((c * CW + col_iota) <= row_pos)
        mb_s[c] = jnp.where(sel, 0.0, NEG)
        return carry + jnp.sum(eqf, axis=1, keepdims=True)
    lax.fori_loop(0, n_all, mask_body, jnp.zeros((T, 1), F32))

    qa = (qa_ref[...] * (dh ** -0.5)).astype(BF16)
    qh = [qa[:, h * dh:(h + 1) * dh] for h in range(ATT_HEADS)]
    m_s[...] = jnp.full(m_s.shape, NEG, F32)
    acc_s[...] = jnp.zeros(acc_s.shape, F32)

    def att_body(c, carry):
        rows = pl.ds(pl.multiple_of(c * CW, CW), CW)
        t = jnp.minimum(qb - 2 * c, 3)
        mb = mb_s[c]
        for g in range(ATT_KV_HEADS):
            kc = kbf_s[g, rows, :]
            vc = vext_s[g, rows, :]
            for r in range(ATT_REP):
                h = g * ATT_REP + r
                s = lax.dot_general(qh[h], kc, (((1,), (1,)), ((), ())),
                                    preferred_element_type=F32)
                s = s + bias_ref[t, h] + mb
                m_old = m_s[h]
                m_new = jnp.maximum(m_old, jnp.max(s, axis=1, keepdims=True))
                alpha = jnp.exp(m_old - m_new)
                p = jnp.exp(s - jnp.concatenate([m_new, m_new], axis=1))
                acc_s[h] = alpha * acc_s[h] + jnp.dot(
                    p.astype(BF16), vc, preferred_element_type=F32)
                m_s[h] = m_new
        return carry
    lax.fori_loop(0, n_all, att_body, 0)

    lane = lax.broadcasted_iota(I32, (T, LANES), 1)
    for hp in range(ATT_HEADS // 2):
        a0 = acc_s[2 * hp]
        a1 = acc_s[2 * hp + 1]
        o0 = a0 / pltpu.roll(a0, dh, 1)
        o1 = a1 / pltpu.roll(a1, dh, 1)
        o_ref[:, hp * LANES:(hp + 1) * LANES] = jnp.where(
            lane < dh, o0, pltpu.roll(o1, dh, 1))


def _dsa(proj, bias_tiles, tri, B, L):
    N = proj.shape[0]
    T = Q_BLOCK
    nqb = L // T
    n_chunks = L // KEY_CHUNK
    ksel = min(TOPK_MAX, L // 4)
    W = ATT_HEADS * ATT_HEAD_DIM
    kern = functools.partial(_dsa_kernel, ksel=ksel, n_chunks=n_chunks)
    return pl.pallas_call(
        kern,
        grid=(B, nqb),
        in_specs=[
            pl.BlockSpec((T, W), lambda b, q: (b * nqb + q, _C_QA // W)),
            pl.BlockSpec((T, 256), lambda b, q: (b * nqb + q, _C_QI // 256)),
            pl.BlockSpec((T, LANES), lambda b, q: (b * nqb + q, _C_KI // LANES)),
            pl.BlockSpec((L, 256), lambda b, q: (b, _C_KA // 256)),
            pl.BlockSpec((L, LANES), lambda b, q: (b, _C_KI // LANES)),
            pl.BlockSpec(bias_tiles.shape, lambda b, q: (0, 0, 0, 0)),
            pl.BlockSpec(tri.shape, lambda b, q: (0, 0)),
        ],
        out_specs=pl.BlockSpec((T, W), lambda b, q: (b * nqb + q, 0)),
        out_shape=jax.ShapeDtypeStruct((N, W), F32),
        scratch_shapes=[
            pltpu.VMEM((ATT_KV_HEADS, L, ATT_HEAD_DIM), BF16),
            pltpu.VMEM((ATT_KV_HEADS, L, LANES), BF16),
            pltpu.VMEM((L, IDX_DIM), BF16),
            pltpu.VMEM((n_chunks, T, KEY_CHUNK), I32),
            pltpu.VMEM((n_chunks, T, KEY_CHUNK), F32),
            pltpu.VMEM((ATT_HEADS, T, LANES), F32),
            pltpu.VMEM((ATT_HEADS, T, LANES), F32),
        ],
        compiler_params=_cparams(("arbitrary", "arbitrary")),
        name="dsa",
    )(proj, proj, proj, proj, proj, bias_tiles, tri)


def _ret_kernel(p_ref, cos_ref, sin_ref, idec_ref, qdec_ref, kdec_ref, cdec_ref,
                g_ref, o_ref, st_s):
    @pl.when(pl.program_id(1) == 0)
    def _():
        st_s[...] = jnp.zeros(st_s.shape, F32)

    d = RET_HEAD_DIM
    W = RET_HEADS * d
    cosf = cos_ref[...]
    sinf = sin_ref[...]
    for h in range(RET_HEADS):
        q = p_ref[:, h * d:(h + 1) * d]
        k = p_ref[:, W + h * d:W + (h + 1) * d]
        v = p_ref[:, 2 * W + h * d:2 * W + (h + 1) * d]
        gate = p_ref[:, 3 * W + h * d:3 * W + (h + 1) * d]
        qr = q * cosf + pltpu.roll(q, d // 2, 1) * sinf
        kr = (k * cosf + pltpu.roll(k, d // 2, 1) * sinf) * (d ** -0.5)
        attn = lax.dot_general(qr, kr, (((1,), (1,)), ((), ())), precision=HIGHEST,
                               preferred_element_type=F32) * idec_ref[h]
        inner = jnp.dot(attn, v, precision=HIGHEST, preferred_element_type=F32)
        st = st_s[h]
        cross = jnp.dot(qr, st, precision=HIGHEST,
                        preferred_element_type=F32) * qdec_ref[h]
        kd = (kr * kdec_ref[h]).T
        st_s[h] = st * cdec_ref[h] + jnp.dot(kd, v, precision=HIGHEST,
                                             preferred_element_type=F32)
        y = inner + cross
        mu = jnp.mean(y, axis=1, keepdims=True)
        yc = y - mu
        var = jnp.mean(yc * yc, axis=1, keepdims=True)
        yn = yc * lax.rsqrt(var + LN_EPS) * g_ref[:, h * d:(h + 1) * d]
        o_ref[:, h * d:(h + 1) * d] = gate * jax.nn.sigmoid(gate) * yn


def _ret(proj, tabs, ret_g, B, L):
    N = proj.shape[0]
    C = RET_CHUNK
    nc = L // C
    d = RET_HEAD_DIM
    W = RET_HEADS * d
    cosf, sinf, idec, qdec, kdec, cdec = tabs
    full3 = lambda b, c: (0, 0, 0)
    return pl.pallas_call(
        _ret_kernel,
        grid=(B, nc),
        in_specs=[
            pl.BlockSpec((C, 4 * W), lambda b, c: (b * nc + c, 0)),
            pl.BlockSpec((C, d), lambda b, c: (c, 0)),
            pl.BlockSpec((C, d), lambda b, c: (c, 0)),
            pl.BlockSpec(idec.shape, full3),
            pl.BlockSpec(qdec.shape, full3),
            pl.BlockSpec(kdec.shape, full3),
            pl.BlockSpec(cdec.shape, full3),
            pl.BlockSpec((1, W), lambda b, c: (0, 0)),
        ],
        out_specs=pl.BlockSpec((C, W), lambda b, c: (b * nc + c, 0)),
        out_shape=jax.ShapeDtypeStruct((N, W), F32),
        scratch_shapes=[pltpu.VMEM((RET_HEADS, d, d), F32)],
        compiler_params=_cparams(("arbitrary", "arbitrary")),
        name="ret",
    )(proj, cosf, sinf, idec, qdec, kdec, cdec, ret_g.reshape(1, W))


def _ret_tables(L):
    d = RET_HEAD_DIM
    H = RET_HEADS
    C = RET_CHUNK
    inv = 1.0 / (ROPE_BASE ** (jnp.arange(0, d, 2, dtype=F32) / d))
    ang = jnp.arange(L, dtype=F32)[:, None] * inv[None, :]
    cos, sin = jnp.cos(ang), jnp.sin(ang)
    cosf = jnp.concatenate([cos, cos], axis=1)
    sinf = jnp.concatenate([-sin, sin], axis=1)
    log_g = jnp.log(1.0 - 2.0 ** (-5.0 - jnp.arange(H, dtype=F32)))
    i = jnp.arange(C, dtype=F32)
    diff = i[:, None] - i[None, :]
    idec = jnp.where(diff[None] >= 0,
                     jnp.exp(jnp.maximum(diff, 0.0)[None] * log_g[:, None, None]), 0.0)
    qdec = jnp.exp((i[None, :] + 1.0) * log_g[:, None])
    kdec = jnp.exp((C - 1.0 - i[None, :]) * log_g[:, None])
    cdec = jnp.exp(C * log_g)
    qdec = jnp.broadcast_to(qdec[:, :, None], (H, C, d))
    kdec = jnp.broadcast_to(kdec[:, :, None], (H, C, d))
    cdec = jnp.broadcast_to(cdec[:, None, None], (H, d, d))
    return cosf, sinf, idec, qdec, kdec, cdec


def _layer_norm(y, g, b):
    mu = jnp.mean(y, axis=1, keepdims=True)
    yc = y - mu
    var = jnp.mean(yc * yc, axis=1, keepdims=True)
    return yc * lax.rsqrt(var + LN_EPS) * g + b


def _post_kernel(a_ref, r_ref, x_ref, wa_ref, wr_ref, ga_ref, scf_ref, shf_ref,
                 lg_ref, lb_ref, wrt_ref, brt_ref,
                 x1_ref, h_ref, idx_ref, gate_ref):
    tm = x_ref.shape[0]
    mix = jnp.dot(a_ref[...].astype(BF16), wa_ref[...], preferred_element_type=F32)
    mix = mix + jnp.dot(r_ref[...].astype(BF16), wr_ref[...], preferred_element_type=F32)
    y = DN_ALPHA * x_ref[...] + (1.0 + ga_ref[...]) * mix
    x1 = _layer_norm(y, lg_ref[...], lb_ref[...])
    x1_ref[...] = x1
    h = x1 * (1.0 + scf_ref[...]) + shf_ref[...]
    for s in range(h.shape[1] // LANES):
        h_ref[pl.ds(s, tm, stride=SUBLANES), :] = h[:, s * LANES:(s + 1) * LANES]

    logits = jnp.dot(h, wrt_ref[...], precision=HIGHEST,
                     preferred_element_type=F32) + brt_ref[...]
    lane = lax.broadcasted_iota(I32, logits.shape, 1)
    idx_out = jnp.zeros(logits.shape, I32)
    val_out = jnp.zeros(logits.shape, F32)
    work = logits
    v0 = None
    for k in range(TOP_K):
        mx = jnp.max(work, axis=1, keepdims=True)
        ix = jnp.min(jnp.where(work == mx, lane, LANES), axis=1, keepdims=True)
        if k == 0:
            v0 = mx
        idx_out = jnp.where(lane == k, ix, idx_out)
        val_out = jnp.where(lane == k, jnp.exp(mx - v0), val_out)
        work = jnp.where(lane == ix, -jnp.inf, work)
    idx_ref[...] = idx_out
    gate_ref[...] = val_out / jnp.sum(val_out, axis=1, keepdims=True)


def _post(attn, ret, xf, w_out_bf, g_a, sc_f, sh_f, ln_g, ln_b, w_rt, b_rt, L, tm=256):
    N, D = xf.shape
    Wa = attn.shape[1]
    Wr = ret.shape[1]
    vec = lambda i: (i * tm // L, 0, 0)
    full = lambda i: (0, 0)
    row = lambda i: (i, 0)
    return pl.pallas_call(
        _post_kernel,
        grid=(N // tm,),
        in_specs=[
            pl.BlockSpec((tm, Wa), row),
            pl.BlockSpec((tm, Wr), row),
            pl.BlockSpec((tm, D), row),
            pl.BlockSpec((Wa, D), lambda i: (0, 0)),
            pl.BlockSpec((Wr, D), lambda i: (Wa // Wr, 0)),
            pl.BlockSpec((None, 1, D), vec),
            pl.BlockSpec((None, 1, D), vec),
            pl.BlockSpec((None, 1, D), vec),
            pl.BlockSpec((1, D), full),
            pl.BlockSpec((1, D), full),
            pl.BlockSpec((D, LANES), full),
            pl.BlockSpec((1, LANES), full),
        ],
        out_specs=[
            pl.BlockSpec((tm, D), row),
            pl.BlockSpec((tm * (D // LANES), LANES), row),
            pl.BlockSpec((tm, LANES), row),
            pl.BlockSpec((tm, LANES), row),
        ],
        out_shape=[
            jax.ShapeDtypeStruct((N, D), F32),
            jax.ShapeDtypeStruct((N * (D // LANES), LANES), F32),
            jax.ShapeDtypeStruct((N, LANES), I32),
            jax.ShapeDtypeStruct((N, LANES), F32),
        ],
        compiler_params=_cparams(("arbitrary",)),
        name="post",
    )(attn, ret, xf, w_out_bf, w_out_bf, g_a, sc_f, sh_f,
      ln_g.reshape(1, D), ln_b.reshape(1, D), w_rt, b_rt)


def _moe_kernel(bexp_ref, nvalid_ref, nused_ref, idx_hbm, gate_ref, h_hbm, wgu_ref, bgu_ref,
                wd_ref, bd_ref, out_hbm, idx_sm, xbuf, ybuf, sem_i, sem_g, sem_s):
    i = pl.program_id(0)
    n_used = nused_ref[0]
    BM = MOE_ROWS
    R = SUBLANES
    n_sub = wgu_ref.shape[0] // LANES
    F = wd_ref.shape[0]

    def idx_copy(blk, s):
        return pltpu.make_async_copy(idx_hbm.at[blk], idx_sm.at[s], sem_i.at[s])

    def issue_gather(s, xs):
        def body(r, carry):
            tok = idx_sm[s, 0, r]
            pltpu.make_async_copy(h_hbm.at[tok],
                                  xbuf.at[xs, pl.ds(pl.multiple_of(r * R, R), R)],
                                  sem_g.at[xs]).start()
            return carry
        lax.fori_loop(0, BM, body, 0)

    def wait_rows(buf, slot, sem, n_rows):
        part = buf.at[slot, pl.ds(0, n_rows * R)]
        pltpu.make_async_copy(part, part, sem.at[slot]).wait()

    def issue_scatter(s, ys, n_rows):
        def body(r, carry):
            dst = idx_sm[s, 1, r]
            pltpu.make_async_copy(ybuf.at[ys, pl.ds(pl.multiple_of(r * R, R), R)],
                                  out_hbm.at[dst], sem_s.at[ys]).start()
            return carry
        lax.fori_loop(0, n_rows, body, 0)

    @pl.when(i < n_used)
    def _active():
        slot = i % 2

        @pl.when(i == 0)
        def _first():
            idx_copy(0, 0).start()
            idx_copy(0, 0).wait()
            issue_gather(0, 0)

            @pl.when(n_used > 1)
            def _():
                idx_copy(1, 1).start()

        @pl.when(i + 1 < n_used)
        def _next():
            s1 = (i + 1) % 3
            idx_copy(i + 1, s1).wait()
            issue_gather(s1, 1 - slot)

        @pl.when(i + 2 < n_used)
        def _next2():
            idx_copy(i + 2, (i + 2) % 3).start()

        wait_rows(xbuf, slot, sem_g, BM)
        x = jnp.concatenate(
            [xbuf[slot, pl.ds(s, BM, stride=R), :] for s in range(n_sub)], axis=1)
        gu = jnp.dot(x.astype(BF16), wgu_ref[...],
                     preferred_element_type=F32) + bgu_ref[...]
        g = jnp.minimum(gu[:, :F], SWIGLU_LIMIT)
        u = jnp.clip(gu[:, F:], -SWIGLU_LIMIT, SWIGLU_LIMIT)
        act = (u + 1.0) * (g * jax.nn.sigmoid(g * SWIGLU_ALPHA))
        y = jnp.dot(act.astype(BF16), wd_ref[...],
                    preferred_element_type=F32) + bd_ref[...]
        y = y * gate_ref[...]

        @pl.when(i >= 2)
        def _():
            wait_rows(ybuf, slot, sem_s, nvalid_ref[i - 2])

        for s in range(y.shape[1] // LANES):
            ybuf[slot, pl.ds(s, BM, stride=R), :] = y[:, s * LANES:(s + 1) * LANES]
        issue_scatter(i % 3, slot, nvalid_ref[i])

        @pl.when(i == n_used - 1)
        def _drain():
            wait_rows(ybuf, slot, sem_s, nvalid_ref[i])

            @pl.when(i >= 1)
            def _():
                wait_rows(ybuf, 1 - slot, sem_s, nvalid_ref[i - 1])


def _moe(h_tiles, idx_blocks, row_gate, bexp, nvalid, nused, wgu, bgu, wd, bd, n_out_rows):
    nb = idx_blocks.shape[0]
    BM = MOE_ROWS
    E, D, F2 = wgu.shape
    F = wd.shape[1]
    R = SUBLANES
    grid_spec = pltpu.PrefetchScalarGridSpec(
        num_scalar_prefetch=3,
        grid=(nb,),
        in_specs=[
            pl.BlockSpec(memory_space=pl.ANY),
            pl.BlockSpec((BM, 1), lambda i, be, nv, nu: (i, 0)),
            pl.BlockSpec(memory_space=pl.ANY),
            pl.BlockSpec((None, D, F2), lambda i, be, nv, nu: (be[i], 0, 0)),
            pl.BlockSpec((None, 1, F2), lambda i, be, nv, nu: (be[i], 0, 0)),
            pl.BlockSpec((None, F, D), lambda i, be, nv, nu: (be[i], 0, 0)),
            pl.BlockSpec((None, 1, D), lambda i, be, nv, nu: (be[i], 0, 0)),
        ],
        out_specs=pl.BlockSpec(memory_space=pl.ANY),
        scratch_shapes=[
            pltpu.SMEM((3, 2, BM), I32),
            pltpu.VMEM((2, BM * R, LANES), F32),
            pltpu.VMEM((2, BM * R, LANES), F32),
            pltpu.SemaphoreType.DMA((3,)),
            pltpu.SemaphoreType.DMA((2,)),
            pltpu.SemaphoreType.DMA((2,)),
        ],
    )
    return pl.pallas_call(
        _moe_kernel,
        grid_spec=grid_spec,
        out_shape=jax.ShapeDtypeStruct((n_out_rows, R, LANES), F32),
        compiler_params=_cparams(("arbitrary",)),
        name="moe",
    )(bexp, nvalid, nused, idx_blocks, row_gate, h_tiles, wgu, bgu, wd, bd)


def _route(top_idx, gates, N):
    BM = MOE_ROWS
    A = N * TOP_K
    nb = A // BM + N_EXPERTS
    P = nb * BM
    e_flat = top_idx.reshape(-1)
    g_flat = gates.reshape(-1)
    order = jnp.argsort(e_flat, stable=True).astype(I32)
    counts = jnp.sum((e_flat[:, None] == jnp.arange(N_EXPERTS, dtype=I32)[None, :]).astype(I32), axis=0)
    starts = jnp.cumsum(counts) - counts
    padded = (counts + BM - 1) // BM * BM
    pends = jnp.cumsum(padded)
    pstarts = pends - padded
    nused = (pends[-1] // BM).astype(I32)
    blk_start = jnp.arange(nb, dtype=I32) * BM
    bexp = jnp.minimum(jnp.searchsorted(pends, blk_start, side='right'),
                       N_EXPERTS - 1).astype(I32)
    last_e = bexp[jnp.maximum(nused - 1, 0)]
    bexp = jnp.where(jnp.arange(nb) < nused, bexp, last_e)
    p = jnp.arange(P, dtype=I32)
    pe = bexp[p // BM]
    off = p - pstarts[pe]
    valid = (off < counts[pe]) & ((p // BM) < nused)
    a = order[jnp.clip(starts[pe] + off, 0, A - 1)]
    row_src = jnp.where(valid, a // TOP_K, 0).astype(I32)
    row_dst = jnp.where(valid, (a % TOP_K) * N + a // TOP_K, 0).astype(I32)
    row_gate = jnp.where(valid, g_flat[a], 0.0).astype(F32)
    idx_blocks = jnp.stack([row_src.reshape(nb, BM), row_dst.reshape(nb, BM)], axis=1)
    nvalid = jnp.sum(valid.reshape(nb, BM).astype(I32), axis=1)
    return idx_blocks, row_gate.reshape(P, 1), bexp, nvalid, nused.reshape(1)


def _final_kernel(x1_ref, y0_ref, y1_ref, y2_ref, y3_ref, gf_ref, lg_ref, lb_ref, o_ref):
    tm, D = x1_ref.shape
    parts = []
    for s in range(D // LANES):
        acc = y0_ref[pl.ds(s, tm, stride=SUBLANES), :]
        for yr in (y1_ref, y2_ref, y3_ref):
            acc = acc + yr[pl.ds(s, tm, stride=SUBLANES), :]
        parts.append(acc)
    ff = jnp.concatenate(parts, axis=1)
    y = DN_ALPHA * x1_ref[...] + (1.0 + gf_ref[...]) * ff
    o_ref[...] = _layer_norm(y, lg_ref[...], lb_ref[...])


def _final(x1, y_tiles, g_f, ln_g, ln_b, L, tm=256):
    N, D = x1.shape
    R = D // LANES
    nblk = N // tm
    yspec = lambda k: pl.BlockSpec((tm * R, LANES), lambda i, k=k: (k * nblk + i, 0))
    return pl.pallas_call(
        _final_kernel,
        grid=(nblk,),
        in_specs=[pl.BlockSpec((tm, D), lambda i: (i, 0)),
                  yspec(0), yspec(1), yspec(2), yspec(3),
                  pl.BlockSpec((None, 1, D), lambda i: (i * tm // L, 0, 0)),
                  pl.BlockSpec((1, D), lambda i: (0, 0)),
                  pl.BlockSpec((1, D), lambda i: (0, 0))],
        out_specs=pl.BlockSpec((tm, D), lambda i: (i, 0)),
        out_shape=jax.ShapeDtypeStruct((N, D), F32),
        compiler_params=_cparams(("arbitrary",)),
        name="final",
    )(x1, y_tiles, y_tiles, y_tiles, y_tiles, g_f, ln_g.reshape(1, D), ln_b.reshape(1, D))


def _t5_bucket(dist):
    max_exact = N_BUCKETS // 2
    n = jnp.maximum(dist, 0)
    nf = jnp.maximum(n, 1).astype(F32)
    large = max_exact + (jnp.log(nf / max_exact) / math.log(MAX_DISTANCE / max_exact)
                         * (N_BUCKETS - max_exact)).astype(I32)
    large = jnp.minimum(large, N_BUCKETS - 1)
    return jnp.where(n < max_exact, n, large)


def _bias_tiles(rel_bias):
    T, CW = Q_BLOCK, KEY_CHUNK
    i = jnp.arange(T, dtype=I32)[:, None]
    j = jnp.arange(CW, dtype=I32)[None, :]
    tiles = []
    for t in range(4):
        dist = T * t + i - j
        b = rel_bias[_t5_bucket(dist)]
        b = jnp.where((dist >= 0)[:, :, None], b, 0.0)
        tiles.append(jnp.moveaxis(b, 2, 0))
    return jnp.stack(tiles, axis=0).astype(F32)


def _pack_w_in(w_in):
    D = w_in.shape[0]
    aw = ATT_HEADS * ATT_HEAD_DIM
    kvw = ATT_KV_HEADS * ATT_HEAD_DIM
    rw = RET_HEADS * RET_HEAD_DIM
    splits = (aw, kvw, kvw, IDX_HEADS * IDX_DIM, IDX_DIM, IDX_HEADS, rw, rw, rw, rw)
    cuts = np.cumsum(splits)[:-1].tolist()
    q_a, k_a, v_a, q_i, k_i, w_i, q_r, k_r, v_r, g_r = jnp.split(w_in, cuts, axis=1)
    pad = jnp.zeros((D, LANES - IDX_DIM - IDX_HEADS), w_in.dtype)
    cat = jnp.concatenate([q_r, k_r, v_r, g_r, q_a, k_a, v_a, q_i, k_i, w_i, pad], axis=1)
    assert cat.shape[1] == _C_TOTAL
    return cat.astype(BF16)


def kernel(x, c, rel_bias, w_ada, b_ada, w_in, ret_norm_g, w_out, ln1_g, ln1_b,
           w_router, b_router, w_gate_up, b_gate_up, w_down, b_down, ln2_g, ln2_b):
    B, L, D = x.shape
    N = B * L
    l = 0
    xf = x.reshape(N, D)

    mod = _ada(c, w_ada[l], b_ada[l])
    sh_a, sc_a, g_a, sh_f, sc_f, g_f = [m.reshape(B, 1, D) for m in jnp.split(mod, 6, axis=1)]

    proj = _inproj(xf, sc_a, sh_a, _pack_w_in(w_in[l]), L)

    tri = jnp.triu(jnp.ones((KEY_CHUNK, KEY_CHUNK), F32), k=1).astype(BF16)
    attn = _dsa(proj, _bias_tiles(rel_bias), tri, B, L)
    ret = _ret(proj, _ret_tables(L), ret_norm_g[l], B, L)

    w_rt = jnp.pad(w_router[l], ((0, 0), (0, LANES - N_EXPERTS)))
    b_rt = jnp.pad(b_router[l], (0, LANES - N_EXPERTS), constant_values=-jnp.inf).reshape(1, LANES)
    x1, h_tiles, ridx, rgate = _post(attn, ret, xf, w_out[l].astype(BF16), g_a, sc_f, sh_f,
                                     ln1_g[l], ln1_b[l], w_rt, b_rt, L)

    idx_blocks, row_gate, bexp, nvalid, nused = _route(ridx[:, :TOP_K], rgate[:, :TOP_K], N)
    E = N_EXPERTS
    y_tiles = _moe(h_tiles.reshape(N, D // LANES, LANES), idx_blocks, row_gate, bexp, nvalid,
                   nused, w_gate_up[l].astype(BF16), b_gate_up[l].reshape(E, 1, -1),
                   w_down[l].astype(BF16), b_down[l].reshape(E, 1, -1),
                   N * TOP_K)
    y2d = y_tiles.reshape(-1, LANES)
    out = _final(x1, y2d, g_f, ln2_g[l], ln2_b[l], L)
    return out.reshape(B, L, D)
```
